```python
import math
import jax, jax.numpy as jnp
from jax import lax
import numpy as np

D_MODEL = 2048
BATCH = 16
SEQ = 256
DEPTH = 4
DEC_BATCH = 2
DEC_SEQ = 1024
PAST_LEN = 512

GRID_W = 64
NORM_EPS = 1e-6
N_BRANCH = 3
RW_HEADS = 16
RW_HEAD_DIM = 64
RW_WIDTH = RW_HEADS * RW_HEAD_DIM
RW_DECAY_LORA = 64
RW_ICLR_LORA = 64
RW_GATE_LORA = 128
RW_GN_EPS = 64e-5
RW_COLS = 3 * RW_WIDTH + 2 * RW_DECAY_LORA + 2 * RW_ICLR_LORA + RW_GATE_LORA
MLA_HEADS = 8
MLA_Q_RANK = 512
MLA_KV_RANK = 512
MLA_NOPE_DIM = 128
MLA_ROPE_DIM = 64
MLA_V_DIM = 128
MLA_WIDTH = MLA_HEADS * MLA_V_DIM
ROPE_FREQS = MLA_ROPE_DIM // 4
ROPE_BASE = 10000.0
Q_BLOCK = 128
S5_GROUPS = 64
S5_GROUP_CH = 16
S5_WIDTH = S5_GROUPS * S5_GROUP_CH
S5_STATE = 64
S5_DT_MIN = 1e-3
S5_DT_MAX = 1e-1
N_IN = RW_COLS + MLA_Q_RANK + MLA_KV_RANK + MLA_ROPE_DIM + S5_WIDTH + N_BRANCH * D_MODEL
PEER_HEADS = 8
PEER_N_KEYS = 128
PEER_N_EXPERTS = PEER_N_KEYS * PEER_N_KEYS
PEER_HALF_DIM = 128
PEER_TOPK = 16
PEER_TOKEN_BLOCK = 128

kernel_name = 'hybrid_dit_rwkv7_mla_s5_peer_step'


def rmsnorm(x, g):
    xf = x.astype(jnp.float32)
    y = xf * lax.rsqrt(jnp.mean(xf * xf, axis=-1, keepdims=True) + NORM_EPS)
    return (y * g.astype(jnp.float32)).astype(x.dtype)


def split_cols(z, sizes):
    out, off = [], 0
    for s in sizes:
        out.append(z[..., off:off + s])
        off += s
    return out


def centred_shift_lerp(z, mu):
    prev = jnp.pad(z[:, :-1], ((0, 0), (1, 0), (0, 0)))
    nxt = jnp.pad(z[:, 1:], ((0, 0), (0, 1), (0, 0)))
    return z + mu * (0.5 * (prev + nxt) - z)


def rope_2d_tables(n_tokens):
    rows = n_tokens // GRID_W
    r_idx, c_idx = jnp.meshgrid(jnp.arange(rows), jnp.arange(GRID_W), indexing='ij')
    pos = jnp.stack([r_idx.reshape(-1), c_idx.reshape(-1)], axis=-1).astype(jnp.float32)
    inv = ROPE_BASE ** (-jnp.arange(ROPE_FREQS, dtype=jnp.float32) / ROPE_FREQS)
    ang = pos[:, :, None] * inv
    return jnp.cos(ang), jnp.sin(ang)


def apply_rope_2d(x, cos, sin):
    xs = x.astype(jnp.float32).reshape(x.shape[:-1] + (2, 2, ROPE_FREQS))
    x1, x2 = xs[..., 0, :], xs[..., 1, :]
    out = jnp.stack([x1 * cos - x2 * sin, x1 * sin + x2 * cos], axis=-2)
    return out.reshape(x.shape).astype(x.dtype)


def rwkv_step(S, inp):
    r, w, k, v, a, b = inp
    sa = jnp.einsum('bdhij,bdhj->bdhi', S, a)
    S = S * w[..., None, :] + sa[..., :, None] * b[..., None, :] + v[..., :, None] * k[..., None, :]
    return S, jnp.einsum('bdhij,bdhj->bdhi', S, r)


def rwkv_mixer(z, lp, s0):
    B, L, _ = z.shape
    H, N = RW_HEADS, RW_HEAD_DIM
    f32 = jnp.float32
    r, k, v, xw, xa, xg = split_cols(z, [RW_WIDTH, RW_WIDTH, RW_WIDTH, 2 * RW_DECAY_LORA, 2 * RW_ICLR_LORA, RW_GATE_LORA])
    xw = xw.reshape(B, L, 2, RW_DECAY_LORA)
    xa = xa.reshape(B, L, 2, RW_ICLR_LORA)
    w_log = -jax.nn.softplus(-(lp['rwkv_w0'] + jnp.einsum('bldr,drc->bldc', jnp.tanh(xw), lp['rwkv_w2']))) - 0.5
    decay = jnp.exp(-jnp.exp(w_log.astype(f32)))
    a_lr = jax.nn.sigmoid(lp['rwkv_a0'] + jnp.einsum('bldr,drc->bldc', xa, lp['rwkv_a2']))
    g = jax.nn.sigmoid(xg) @ lp['rwkv_g2']

    def heads(t):
        return t.reshape(t.shape[:-1] + (H, N))

    kk = heads(k * lp['rwkv_k_k']).astype(f32)
    kk = kk * lax.rsqrt(jnp.maximum(jnp.sum(kk * kk, axis=-1, keepdims=True), 1e-24))
    k_dir = heads(k[:, :, None, :] * (1 + (a_lr - 1) * lp['rwkv_k_a']))
    b_dir = kk[:, :, None] * heads(a_lr)
    r_h, v_h = heads(r), heads(v)

    def dirs_shared(t):
        return jnp.stack([t, jnp.flip(t, 1)], axis=2).astype(f32).transpose(1, 0, 2, 3, 4)

    def dirs_own(t):
        return jnp.stack([t[:, :, 0], jnp.flip(t[:, :, 1], 1)], axis=2).astype(f32).transpose(1, 0, 2, 3, 4)

    xs = (dirs_shared(r_h), dirs_own(heads(decay)), dirs_own(k_dir), dirs_shared(v_h), dirs_shared(-kk), dirs_own(b_dir))
    s_fin, ys = lax.scan(rwkv_step, s0.astype(f32), xs)
    ys = ys.transpose(1, 0, 2, 3, 4)
    y = ys[:, :, 0] + jnp.flip(ys[:, :, 1], 1)
    mean = jnp.mean(y, axis=-1, keepdims=True)
    var = jnp.mean(jnp.square(y - mean), axis=-1, keepdims=True)
    y = ((y - mean) * lax.rsqrt(var + RW_GN_EPS)).reshape(B, L, RW_WIDTH) * lp['rwkv_ln_w'] + lp['rwkv_ln_b']
    bonus = jnp.sum(jnp.sum(r_h[:, :, None] * k_dir * lp['rwkv_r_k'], axis=-1, keepdims=True), axis=2) * v_h
    out = ((y + bonus.reshape(B, L, RW_WIDTH)) * g).astype(z.dtype)
    return out, s_fin.astype(z.dtype)


def block_attention(q_nope, q_rope, k_nope, k_rope, v):
    B, Lq, H, _ = q_nope.shape
    nb = Lq // Q_BLOCK
    scale = (MLA_NOPE_DIM + MLA_ROPE_DIM) ** -0.5

    def to_blocks(t):
        return t.reshape((B, nb, Q_BLOCK) + t.shape[2:]).swapaxes(0, 1)

    def one_block(qs):
        qn, qr = qs
        s = jnp.einsum('bqhd,bkhd->bhqk', qn, k_nope) + jnp.einsum('bqhd,bkd->bhqk', qr, k_rope)
        p = jax.nn.softmax(s.astype(jnp.float32) * scale, axis=-1).astype(v.dtype)
        return jnp.einsum('bhqk,bkhd->bqhd', p, v)

    o = lax.map(one_block, (to_blocks(q_nope), to_blocks(q_rope)))
    return o.swapaxes(0, 1).reshape(B, Lq, H, MLA_V_DIM)


def mla_mixer(z_q, z_kv, z_kr, lp, ctx_kv, rope):
    B, L, _ = z_q.shape
    q = (rmsnorm(z_q, lp['mla_q_norm']) @ lp['mla_w_uq']).reshape(B, L, MLA_HEADS, MLA_NOPE_DIM + MLA_ROPE_DIM)
    q_nope, q_rope = q[..., :MLA_NOPE_DIM], q[..., MLA_NOPE_DIM:]
    ckv = rmsnorm(z_kv, lp['mla_kv_norm'])
    k_rope = z_kr
    if rope is not None:
        cos, sin = rope
        q_rope = apply_rope_2d(q_rope, cos[:, None], sin[:, None])
        k_rope = apply_rope_2d(k_rope, cos, sin)
    ckv_all, kr_all = ckv, k_rope
    if ctx_kv is not None:
        ckv_all = jnp.concatenate([ctx_kv[0].astype(ckv.dtype), ckv], axis=1)
        kr_all = jnp.concatenate([ctx_kv[1].astype(k_rope.dtype), k_rope], axis=1)
    Lk = ckv_all.shape[1]
    k_nope = (ckv_all @ lp['mla_w_uk']).reshape(B, Lk, MLA_HEADS, MLA_NOPE_DIM)
    v = (ckv_all @ lp['mla_w_uv']).reshape(B, Lk, MLA_HEADS, MLA_V_DIM)
    o = block_attention(q_nope, q_rope, k_nope, kr_all, v)
    return o.reshape(B, L, MLA_WIDTH), ckv, k_rope


def complex_affine_combine(e1, e2):
    a1r, a1i, b1r, b1i = e1
    a2r, a2i, b2r, b2i = e2
    return (a2r * a1r - a2i * a1i, a2r * a1i + a2i * a1r,
            a2r * b1r - a2i * b1i + b2r, a2r * b1i + a2i * b1r + b2i)


def s5_mixer(z, lp, h0_re, h0_im):
    B, L, _ = z.shape
    f32 = jnp.float32
    u = z.astype(f32).reshape(B, L, S5_GROUPS, S5_GROUP_CH)
    a_re, a_im = lp['s5_a_re'].astype(f32), lp['s5_a_im'].astype(f32)
    dt = jnp.exp(lp['s5_log_step'].astype(f32))[..., None]
    mag = jnp.exp(a_re * dt)
    lb_re, lb_im = mag * jnp.cos(a_im * dt), mag * jnp.sin(a_im * dt)
    den = a_re * a_re + a_im * a_im
    nr, ni = lb_re - 1.0, lb_im
    coef_re = (nr * a_re + ni * a_im) / den
    coef_im = (ni * a_re - nr * a_im) / den
    b_re, b_im = lp['s5_b_re'].astype(f32), lp['s5_b_im'].astype(f32)
    bb_re = coef_re[..., None] * b_re - coef_im[..., None] * b_im
    bb_im = coef_re[..., None] * b_im + coef_im[..., None] * b_re
    u_dir = jnp.stack([u, jnp.flip(u, 1)], axis=0)
    bu_re = jnp.einsum('dblgc,dgpc->dblgp', u_dir, bb_re)
    bu_im = jnp.einsum('dblgc,dgpc->dblgp', u_dir, bb_im)
    h0r = h0_re.astype(f32).swapaxes(0, 1)
    h0i = h0_im.astype(f32).swapaxes(0, 1)
    bu_re = bu_re.at[:, :, 0].add(lb_re[:, None] * h0r - lb_im[:, None] * h0i)
    bu_im = bu_im.at[:, :, 0].add(lb_re[:, None] * h0i + lb_im[:, None] * h0r)
    ar = jnp.broadcast_to(lb_re[:, None, None], bu_re.shape)
    ai = jnp.broadcast_to(lb_im[:, None, None], bu_im.shape)
    _, _, h_re, h_im = lax.associative_scan(complex_affine_combine, (ar, ai, bu_re, bu_im), axis=2)
    fin_re = h_re[:, :, -1].swapaxes(0, 1)
    fin_im = h_im[:, :, -1].swapaxes(0, 1)
    h_re = jnp.stack([h_re[0], jnp.flip(h_re[1], 1)], axis=0)
    h_im = jnp.stack([h_im[0], jnp.flip(h_im[1], 1)], axis=0)
    y = (jnp.einsum('dblgp,dgcp->blgc', h_re, lp['s5_c_re'].astype(f32))
         - jnp.einsum('dblgp,dgcp->blgc', h_im, lp['s5_c_im'].astype(f32)))
    y = y.reshape(B, L, S5_WIDTH) + lp['s5_d'].astype(f32) * z.astype(f32)
    y = jax.nn.gelu(y).astype(z.dtype)
    out = (y @ lp['s5_glu_w1']) * jax.nn.sigmoid(y @ lp['s5_glu_w2'])
    return out, fin_re.astype(z.dtype), fin_im.astype(z.dtype)


def peer_ffn(h, lp):
    B, L, D = h.shape
    T = B * L
    x = h.reshape(T, D)
    q = (x @ lp['peer_wq']).reshape(T, PEER_HEADS, 2, PEER_HALF_DIM)
    s = jnp.einsum('thsd,hsnd->thsn', q, lp['peer_keys']).astype(jnp.float32)
    sv, si = lax.top_k(s, PEER_TOPK)
    cand = (sv[:, :, 0, :, None] + sv[:, :, 1, None, :]).reshape(T, PEER_HEADS, PEER_TOPK * PEER_TOPK)
    cv, ci = lax.top_k(cand, PEER_TOPK)
    i1 = jnp.take_along_axis(si[:, :, 0], ci // PEER_TOPK, axis=-1)
    i2 = jnp.take_along_axis(si[:, :, 1], ci % PEER_TOPK, axis=-1)
    nb = T // PEER_TOKEN_BLOCK
    experts = (i1 * PEER_N_KEYS + i2).reshape(nb, PEER_TOKEN_BLOCK, PEER_HEADS * PEER_TOPK)
    gates = jax.nn.softmax(cv, axis=-1).astype(h.dtype).reshape(nb, PEER_TOKEN_BLOCK, PEER_HEADS * PEER_TOPK)
    u_tab, v_tab = lp['peer_u'], lp['peer_v']

    def one_block(args):
        xb, eb, gb = args
        act = jax.nn.gelu(jnp.einsum('td,tkd->tk', xb, u_tab[eb])) * gb
        return jnp.einsum('tk,tkd->td', act, v_tab[eb])

    out = lax.map(one_block, (x.reshape(nb, PEER_TOKEN_BLOCK, D), experts, gates))
    return out.reshape(B, L, D)


def trunk_layer(x, cond, lp, ctx, rope):
    B, L, _ = x.shape
    mod = (cond @ lp['ada_w'] + lp['ada_b'])[:, None, :]
    sh1, sc1, g1, sh2, sc2, g2 = jnp.split(mod, 6, axis=-1)
    h = rmsnorm(x, lp['norm1_g']) * (1 + sc1) + sh1
    z = h @ lp['w_in']
    z_rw, z_q, z_kv, z_kr, z_s5, z_gate = split_cols(
        z, [RW_COLS, MLA_Q_RANK, MLA_KV_RANK, MLA_ROPE_DIM, S5_WIDTH, N_BRANCH * D_MODEL])
    if ctx is None:
        ctx_kv = None
        rw0 = jnp.zeros((B, 2, RW_HEADS, RW_HEAD_DIM, RW_HEAD_DIM), jnp.float32)
        s5_0_re = jnp.zeros((B, 2, S5_GROUPS, S5_STATE), jnp.float32)
        s5_0_im = s5_0_re
    else:
        ckv_c, kr_c, rw0, s5_0_re, s5_0_im = ctx
        ctx_kv = (ckv_c, kr_c)
    rw_out, rw_fin = rwkv_mixer(centred_shift_lerp(z_rw, lp['rwkv_mu']), lp, rw0)
    mla_out, ckv, k_rope = mla_mixer(z_q, z_kv, z_kr, lp, ctx_kv, rope)
    s5_out, s5_fin_re, s5_fin_im = s5_mixer(z_s5, lp, s5_0_re, s5_0_im)
    g_rw, g_mla, g_s5 = jnp.split(jax.nn.sigmoid(z_gate), N_BRANCH, axis=-1)
    merged = (g_rw * (rw_out @ lp['w_branch_rwkv'])
              + g_mla * (mla_out @ lp['w_branch_mla'])
              + g_s5 * (s5_out @ lp['w_branch_s5']))
    x = x + g1 * (merged @ lp['w_out'])
    h2 = rmsnorm(x, lp['norm2_g']) * (1 + sc2) + sh2
    x = x + g2 * peer_ffn(h2, lp)
    return x, (ckv, k_rope, rw_fin, s5_fin_re, s5_fin_im)


def setup_inputs(seed: int = 0) -> dict:
    key = jax.random.key(seed)
    keys = iter(jax.random.split(key, 64))
    f32 = jnp.float32
    D = D_MODEL

    def nrm(shape, scale):
        return jax.random.normal(next(keys), shape, f32) * scale

    def unif(shape, lo, hi):
        return jax.random.uniform(next(keys), shape, f32, lo, hi)

    def gain(shape):
        return 1.0 + nrm(shape, 0.02)

    return {
        'x_prompt': nrm((BATCH, SEQ, D), 1.0),
        'x_sample': nrm((DEC_BATCH, DEC_SEQ, D), 1.0),
        'cache_mla_ckv': nrm((DEC_BATCH, DEPTH, PAST_LEN, MLA_KV_RANK), 1.0),
        'cache_mla_krope': nrm((DEC_BATCH, DEPTH, PAST_LEN, MLA_ROPE_DIM), 1.0),
        'state_rwkv': nrm((DEC_BATCH, DEPTH, 2, RW_HEADS, RW_HEAD_DIM, RW_HEAD_DIM), 1.0),
        'state_s5_re': nrm((DEC_BATCH, DEPTH, 2, S5_GROUPS, S5_STATE), 0.1),
        'state_s5_im': nrm((DEC_BATCH, DEPTH, 2, S5_GROUPS, S5_STATE), 0.1),
        'c': nrm((DEC_BATCH, D), 1.0),
        'c_ctx': nrm((D,), 1.0),
        'ada_w': nrm((DEPTH, D, 6 * D), 0.5 * D ** -0.5),
        'ada_b': nrm((DEPTH, 6 * D), 0.02),
        'norm1_g': gain((DEPTH, D)),
        'norm2_g': gain((DEPTH, D)),
        'w_in': nrm((DEPTH, D, N_IN), D ** -0.5),
        'rwkv_mu': unif((DEPTH, RW_COLS), 0.0, 1.0),
        'rwkv_w0': unif((DEPTH, 2, RW_WIDTH), -6.0, -0.5),
        'rwkv_w2': nrm((DEPTH, 2, RW_DECAY_LORA, RW_WIDTH), 0.1 * RW_DECAY_LORA ** -0.5),
        'rwkv_a0': nrm((DEPTH, 2, RW_WIDTH), 0.5),
        'rwkv_a2': nrm((DEPTH, 2, RW_ICLR_LORA, RW_WIDTH), 0.5 * RW_ICLR_LORA ** -0.5),
        'rwkv_g2': nrm((DEPTH, RW_GATE_LORA, RW_WIDTH), RW_GATE_LORA ** -0.5),
        'rwkv_k_k': 0.85 + nrm((DEPTH, RW_WIDTH), 0.05),
        'rwkv_k_a': 1.0 + nrm((DEPTH, RW_WIDTH), 0.05),
        'rwkv_r_k': nrm((DEPTH, RW_HEADS, RW_HEAD_DIM), 0.1),
        'rwkv_ln_w': gain((DEPTH, RW_WIDTH)),
        'rwkv_ln_b': nrm((DEPTH, RW_WIDTH), 0.02),
        'mla_q_norm': gain((DEPTH, MLA_Q_RANK)),
        'mla_w_uq': nrm((DEPTH, MLA_Q_RANK, MLA_HEADS * (MLA_NOPE_DIM + MLA_ROPE_DIM)), MLA_Q_RANK ** -0.5),
        'mla_kv_norm': gain((DEPTH, MLA_KV_RANK)),
        'mla_w_uk': nrm((DEPTH, MLA_KV_RANK, MLA_HEADS * MLA_NOPE_DIM), MLA_KV_RANK ** -0.5),
        'mla_w_uv': nrm((DEPTH, MLA_KV_RANK, MLA_WIDTH), MLA_KV_RANK ** -0.5),
        's5_a_re': -0.5 + nrm((DEPTH, 2, S5_GROUPS, S5_STATE), 0.01),
        's5_a_im': jnp.pi * jnp.arange(S5_STATE, dtype=f32) + nrm((DEPTH, 2, S5_GROUPS, S5_STATE), 0.01),
        's5_log_step': unif((DEPTH, 2, S5_GROUPS), math.log(S5_DT_MIN), math.log(S5_DT_MAX)),
        's5_b_re': nrm((DEPTH, S5_GROUPS, S5_STATE, S5_GROUP_CH), (2 * S5_GROUP_CH) ** -0.5),
        's5_b_im': nrm((DEPTH, S5_GROUPS, S5_STATE, S5_GROUP_CH), (2 * S5_GROUP_CH) ** -0.5),
        's5_c_re': nrm((DEPTH, 2, S5_GROUPS, S5_GROUP_CH, S5_STATE), S5_STATE ** -0.5),
        's5_c_im': nrm((DEPTH, 2, S5_GROUPS, S5_GROUP_CH, S5_STATE), S5_STATE ** -0.5),
        's5_d': nrm((DEPTH, S5_WIDTH), 1.0),
        's5_glu_w1': nrm((DEPTH, S5_WIDTH, S5_WIDTH), S5_WIDTH ** -0.5),
        's5_glu_w2': nrm((DEPTH, S5_WIDTH, S5_WIDTH), S5_WIDTH ** -0.5),
        'w_branch_rwkv': nrm((DEPTH, RW_WIDTH, D), RW_WIDTH ** -0.5),
        'w_branch_mla': nrm((DEPTH, MLA_WIDTH, D), MLA_WIDTH ** -0.5),
        'w_branch_s5': nrm((DEPTH, S5_WIDTH, D), S5_WIDTH ** -0.5),
        'w_out': nrm((DEPTH, D, D), D ** -0.5),
        'peer_wq': nrm((DEPTH, D, PEER_HEADS * 2 * PEER_HALF_DIM), D ** -0.5),
        'peer_keys': nrm((DEPTH, PEER_HEADS, 2, PEER_N_KEYS, PEER_HALF_DIM), PEER_HALF_DIM ** -0.5),
        'peer_u': nrm((DEPTH, PEER_N_EXPERTS, D), D ** -0.5),
        'peer_v': nrm((DEPTH, PEER_N_EXPERTS, D), 0.5),
        'final_norm_g': gain((D,)),
    }


def reference(x_prompt, x_sample, cache_mla_ckv, cache_mla_krope, state_rwkv, state_s5_re, state_s5_im,
              c, c_ctx, ada_w, ada_b, norm1_g, norm2_g, w_in,
              rwkv_mu, rwkv_w0, rwkv_w2, rwkv_a0, rwkv_a2, rwkv_g2, rwkv_k_k, rwkv_k_a, rwkv_r_k,
              rwkv_ln_w, rwkv_ln_b,
              mla_q_norm, mla_w_uq, mla_kv_norm, mla_w_uk, mla_w_uv,
              s5_a_re, s5_a_im, s5_log_step, s5_b_re, s5_b_im, s5_c_re, s5_c_im, s5_d,
              s5_glu_w1, s5_glu_w2,
              w_branch_rwkv, w_branch_mla, w_branch_s5, w_out,
              peer_wq, peer_keys, peer_u, peer_v, final_norm_g):
    cond_ctx = jax.nn.silu(c_ctx)[None, :]
    cond_lat = jax.nn.silu(c)
    rope = rope_2d_tables(x_sample.shape[1])
    h_ctx, h_lat = x_prompt, x_sample
    ckv_l, kr_l, rw_l, s5r_l, s5i_l = [], [], [], [], []
    for l in range(DEPTH):
        lp = {
            'ada_w': ada_w[l], 'ada_b': ada_b[l], 'norm1_g': norm1_g[l], 'norm2_g': norm2_g[l],
            'w_in': w_in[l],
            'rwkv_mu': rwkv_mu[l], 'rwkv_w0': rwkv_w0[l], 'rwkv_w2': rwkv_w2[l],
            'rwkv_a0': rwkv_a0[l], 'rwkv_a2': rwkv_a2[l], 'rwkv_g2': rwkv_g2[l],
            'rwkv_k_k': rwkv_k_k[l], 'rwkv_k_a': rwkv_k_a[l], 'rwkv_r_k': rwkv_r_k[l],
            'rwkv_ln_w': rwkv_ln_w[l], 'rwkv_ln_b': rwkv_ln_b[l],
            'mla_q_norm': mla_q_norm[l], 'mla_w_uq': mla_w_uq[l], 'mla_kv_norm': mla_kv_norm[l],
            'mla_w_uk': mla_w_uk[l], 'mla_w_uv': mla_w_uv[l],
            's5_a_re': s5_a_re[l], 's5_a_im': s5_a_im[l], 's5_log_step': s5_log_step[l],
            's5_b_re': s5_b_re[l], 's5_b_im': s5_b_im[l], 's5_c_re': s5_c_re[l], 's5_c_im': s5_c_im[l],
            's5_d': s5_d[l], 's5_glu_w1': s5_glu_w1[l], 's5_glu_w2': s5_glu_w2[l],
            'w_branch_rwkv': w_branch_rwkv[l], 'w_branch_mla': w_branch_mla[l],
            'w_branch_s5': w_branch_s5[l], 'w_out': w_out[l],
            'peer_wq': peer_wq[l], 'peer_keys': peer_keys[l], 'peer_u': peer_u[l], 'peer_v': peer_v[l],
        }
        h_ctx, (ckv, krope, rw_fin, s5_re, s5_im) = trunk_layer(h_ctx, cond_ctx, lp, None, None)
        ckv_l.append(ckv)
        kr_l.append(krope)
        rw_l.append(rw_fin)
        s5r_l.append(s5_re)
        s5i_l.append(s5_im)
        lat_ctx = (cache_mla_ckv[:, l], cache_mla_krope[:, l], state_rwkv[:, l], state_s5_re[:, l], state_s5_im[:, l])
        h_lat, _ = trunk_layer(h_lat, cond_lat, lp, lat_ctx, rope)
    y_prompt = rmsnorm(h_ctx, final_norm_g)
    y_sample = rmsnorm(h_lat, final_norm_g)
    new_mla_ckv = jnp.stack(ckv_l, axis=1)
    new_mla_krope = jnp.stack(kr_l, axis=1)
    new_rwkv = jnp.stack(rw_l, axis=1)
    new_s5_re = jnp.stack(s5r_l, axis=1)
    new_s5_im = jnp.stack(s5i_l, axis=1)
    return (y_prompt, y_sample, new_mla_ckv, new_mla_krope, new_rwkv, new_s5_re, new_s5_im)
```

```python
import functools
import math

import jax
import jax.numpy as jnp
from jax import lax
from jax.experimental import pallas as pl
from jax.experimental.pallas import tpu as pltpu

F32, BF16 = jnp.float32, jnp.bfloat16

D_MODEL = 2048
GRID_W = 64
NORM_EPS = 1e-6
RW_HEADS = 16
RW_HEAD_DIM = 64
RW_WIDTH = RW_HEADS * RW_HEAD_DIM
RW_LORA = 64
RW_GATE_LORA = 128
RW_GN_EPS = 64e-5
RW_COLS = 3 * RW_WIDTH + 4 * RW_LORA + RW_GATE_LORA
MLA_HEADS = 8
MLA_RANK = 512
MLA_NOPE = 128
MLA_ROPE = 64
MLA_V = 128
ROPE_FREQS = MLA_ROPE // 4
ROPE_BASE = 10000.0
S5_GROUPS = 64
S5_CH = 16
S5_WIDTH = S5_GROUPS * S5_CH
S5_STATE = 64
S5_LANES = S5_GROUPS * S5_STATE
PEER_HEADS = 8
PEER_KEYS = 128
PEER_HALF = 128
PEER_TOPK = 16

COLS_A = RW_COLS + 2 * MLA_RANK + 2 * MLA_ROPE
S5_COL0 = RW_COLS + 2 * MLA_RANK + MLA_ROPE
GATE_COL0 = S5_COL0 + S5_WIDTH

TM = 1024
RW_CHUNK = 64
S5_STREAMS = 8
S5_TSTEPS = 64
TOPK_TT = 512
PEER_TT = 1024
PEER_TE = 512
VMEM_LIMIT = 56 * 1024 * 1024


def _cp(*sem):
    return pltpu.CompilerParams(dimension_semantics=sem, vmem_limit_bytes=VMEM_LIMIT)


def _dot(a, b):
    return jnp.dot(a.astype(BF16), b.astype(BF16), preferred_element_type=F32)


def _dot_nt(a, b):
    return lax.dot_general(a.astype(BF16), b.astype(BF16), (((1,), (1,)), ((), ())),
                           preferred_element_type=F32)


def _dot_tn(a, b):
    return lax.dot_general(a.astype(BF16), b.astype(BF16), (((0,), (0,)), ((), ())),
                           preferred_element_type=F32)


def _dot_split(x, e):
    hi = x.astype(BF16)
    lo = (x - hi.astype(F32)).astype(BF16)
    return (jnp.dot(hi, e, preferred_element_type=F32)
            + jnp.dot(lo, e, preferred_element_type=F32))


def _dot_hi_lo(e, x):
    hi = x.astype(BF16)
    lo = (x - hi.astype(F32)).astype(BF16)
    return (jnp.dot(e, hi, preferred_element_type=F32)
            + jnp.dot(e, lo, preferred_element_type=F32))


def _dot_split3(e, x):
    hi = x.astype(BF16)
    r1 = x - hi.astype(F32)
    mid = r1.astype(BF16)
    lo = (r1 - mid.astype(F32)).astype(BF16)
    return (jnp.dot(e, hi, preferred_element_type=F32)
            + jnp.dot(e, mid, preferred_element_type=F32)
            + jnp.dot(e, lo, preferred_element_type=F32))


def _call_into(kernel_fn, args, out_buf, *, in_specs, **kw):
    if out_buf is None:
        return pl.pallas_call(kernel_fn, in_specs=in_specs, **kw)(*args)
    n = len(args)

    def body(*refs):
        kernel_fn(*refs[:n], *refs[n + 1:])

    return pl.pallas_call(body, in_specs=list(in_specs) + [pl.BlockSpec(memory_space=pl.ANY)],
                          input_output_aliases={n: 0}, **kw)(*args, out_buf)


def _ada_kernel(c_ref, w_ref, b_ref, o_ref):
    c = c_ref[...]
    cond = c * jax.nn.sigmoid(c)
    o_ref[...] = _dot(cond, w_ref[...]) + b_ref[...]


def ada_mod(cond8, ada_w, ada_b):
    depth, d, n = ada_w.shape
    tn = 1536
    return pl.pallas_call(
        _ada_kernel, grid=(depth, n // tn),
        in_specs=[pl.BlockSpec((8, d), lambda l, j: (0, 0)),
                  pl.BlockSpec((None, d, tn), lambda l, j: (l, 0, j)),
                  pl.BlockSpec((None, 1, tn), lambda l, j: (l, 0, j))],
        out_specs=pl.BlockSpec((None, 8, tn), lambda l, j: (l, 0, j)),
        out_shape=jax.ShapeDtypeStruct((depth, 8, n), F32),
        compiler_params=_cp("parallel", "parallel"), name="ada_mod",
    )(cond8, ada_w, ada_b.reshape(depth, 1, n))


def _resid_norm_kernel(x_ref, d_ref, gate_ref, g_ref, sc_ref, sh_ref, xo_ref, h_ref):
    x = x_ref[...] + gate_ref[...] * d_ref[...].astype(F32)
    xo_ref[...] = x
    y = x * lax.rsqrt(jnp.mean(x * x, axis=-1, keepdims=True) + NORM_EPS)
    y = y * g_ref[...]
    h_ref[...] = (y * (1.0 + sc_ref[...]) + sh_ref[...]).astype(h_ref.dtype)


def _norm_kernel(x_ref, g_ref, sc_ref, sh_ref, h_ref):
    x = x_ref[...]
    y = x * lax.rsqrt(jnp.mean(x * x, axis=-1, keepdims=True) + NORM_EPS)
    y = y * g_ref[...]
    h_ref[...] = (y * (1.0 + sc_ref[...]) + sh_ref[...]).astype(h_ref.dtype)


def _mod_spec(row_of_tile, chunk, d):
    return pl.BlockSpec((None, 1, d), lambda i: (row_of_tile(i), 0, chunk))


def norm_mod(x, g, mod, row_of_tile, sc_chunk, sh_chunk, out_dtype, tm):
    t, d = x.shape
    return pl.pallas_call(
        _norm_kernel, grid=(t // tm,),
        in_specs=[pl.BlockSpec((tm, d), lambda i: (i, 0)),
                  pl.BlockSpec((1, d), lambda i: (0, 0)),
                  _mod_spec(row_of_tile, sc_chunk, d),
                  _mod_spec(row_of_tile, sh_chunk, d)],
        out_specs=pl.BlockSpec((tm, d), lambda i: (i, 0)),
        out_shape=jax.ShapeDtypeStruct((t, d), out_dtype),
        compiler_params=_cp("parallel"), name="norm_mod",
    )(x, g.reshape(1, d), mod, mod)


def resid_norm_mod(x, delta, mod_gate, gate_chunk, g, mod, row_of_tile, sc_chunk, sh_chunk,
                   out_dtype, tm):
    t, d = x.shape
    return pl.pallas_call(
        _resid_norm_kernel, grid=(t // tm,),
        in_specs=[pl.BlockSpec((tm, d), lambda i: (i, 0)),
                  pl.BlockSpec((tm, d), lambda i: (i, 0)),
                  _mod_spec(row_of_tile, gate_chunk, d),
                  pl.BlockSpec((1, d), lambda i: (0, 0)),
                  _mod_spec(row_of_tile, sc_chunk, d),
                  _mod_spec(row_of_tile, sh_chunk, d)],
        out_specs=[pl.BlockSpec((tm, d), lambda i: (i, 0)),
                   pl.BlockSpec((tm, d), lambda i: (i, 0))],
        out_shape=[jax.ShapeDtypeStruct((t, d), F32), jax.ShapeDtypeStruct((t, d), out_dtype)],
        compiler_params=_cp("parallel"), name="resid_norm_mod",
    )(x, delta, mod_gate, g.reshape(1, d), mod, mod)


def _mm_kernel(x_ref, w_ref, o_ref, *, act):
    acc = _dot(x_ref[...], w_ref[...])
    if act == "sigmoid":
        acc = jax.nn.sigmoid(acc)
    o_ref[...] = acc.astype(o_ref.dtype)


def matmul(x, w, *, tm, tn, out_dtype, act=None, layer=None, row0=0, rows=None, name="matmul"):
    k = x.shape[1]
    t = x.shape[0] if rows is None else rows
    n = w.shape[-1]
    i0 = row0 // tm
    if layer is None:
        w_spec = pl.BlockSpec((k, tn), lambda i, j: (0, j))
    else:
        w_spec = pl.BlockSpec((None, k, tn), lambda i, j: (layer, 0, j))
    return pl.pallas_call(
        functools.partial(_mm_kernel, act=act), grid=(t // tm, n // tn),
        in_specs=[pl.BlockSpec((tm, k), lambda i, j: (i0 + i, 0)), w_spec],
        out_specs=pl.BlockSpec((tm, tn), lambda i, j: (i, j)),
        out_shape=jax.ShapeDtypeStruct((t, n), out_dtype),
        compiler_params=_cp("parallel", "parallel"), name=name,
    )(x, w)


def _mm_wt_kernel(x_ref, w_ref, o_ref, *, act):
    acc = _dot_nt(x_ref[...], w_ref[0])
    if act == "sigmoid":
        acc = jax.nn.sigmoid(acc)
    o_ref[...] = acc.astype(o_ref.dtype)


def matmul_wt(x, w_t, layer, col0, n, *, tm, tn, out_dtype, act=None, row0=0, rows=None, name="matmul_wt"):
    k = x.shape[1]
    t = x.shape[0] if rows is None else rows
    i0 = row0 // tm
    assert col0 % 8 == 0 and n % tn == 0
    return pl.pallas_call(
        functools.partial(_mm_wt_kernel, act=act), grid=(t // tm, n // tn),
        in_specs=[pl.BlockSpec((tm, k), lambda i, j: (i0 + i, 0)),
                  pl.BlockSpec((pl.Element(1), pl.Element(tn), pl.Element(k)),
                               lambda i, j: (layer, pl.multiple_of(col0 + j * tn, 8), 0))],
        out_specs=pl.BlockSpec((tm, tn), lambda i, j: (i, j)),
        out_shape=jax.ShapeDtypeStruct((t, n), out_dtype),
        compiler_params=_cp("parallel", "parallel"), name=name,
    )(x, w_t)


def _mm_lerp_kernel(x_ref, w_ref, mu_ref, o_ref, *, n_ctx_tiles, l_ctx, l_lat):
    i = pl.program_id(0)
    acc = _dot_nt(x_ref[...], w_ref[...])
    tm = acc.shape[0]
    row = lax.broadcasted_iota(jnp.int32, (tm, 1), 0)
    seq_len = jnp.where(i < n_ctx_tiles, l_ctx, l_lat)
    pos = row & (seq_len - 1)
    prev = jnp.where(pos == 0, 0.0, pltpu.roll(acc, 1, 0))
    nxt = jnp.where(pos == seq_len - 1, 0.0, pltpu.roll(acc, tm - 1, 0))
    o_ref[...] = acc + mu_ref[...] * (0.5 * (prev + nxt) - acc)


def matmul_in_a(h, w_in_t, layer, mu_pad, *, n_ctx_tiles, l_ctx, l_lat, tm, tn):
    t, k = h.shape
    return pl.pallas_call(
        functools.partial(_mm_lerp_kernel, n_ctx_tiles=n_ctx_tiles, l_ctx=l_ctx, l_lat=l_lat),
        grid=(t // tm, COLS_A // tn),
        in_specs=[pl.BlockSpec((tm, k), lambda i, j: (i, 0)),
                  pl.BlockSpec((None, tn, k), lambda i, j: (layer, j, 0)),
                  pl.BlockSpec((1, tn), lambda i, j: (0, j))],
        out_specs=pl.BlockSpec((tm, tn), lambda i, j: (i, j)),
        out_shape=jax.ShapeDtypeStruct((t, COLS_A), F32),
        compiler_params=_cp("parallel", "parallel"), name="matmul_in_a",
    )(h, w_in_t, mu_pad)


def _softplus(x):
    return jnp.maximum(x, 0.0) + jnp.log1p(jnp.exp(-jnp.abs(x)))


def _rwkv_prep_kernel(z_ref, w0_ref, w2_ref, a0_ref, a2_ref, g2_ref, kk_ref, ka_ref, rk_ref, e_ref,
                      r_ref, v_ref, a_ref, lw_ref, kd_ref, bd_ref, bonus_ref, g_ref):
    w = RW_WIDTH
    r = z_ref[:, 0:w]
    k = z_ref[:, w:2 * w]
    v = z_ref[:, 2 * w:3 * w]
    lora = z_ref[:, 3 * w:3 * w + 3 * 128]
    xg = lora[:, 256:384]
    e = e_ref[...]
    kk = k * kk_ref[...]
    ss = _dot_split(kk * kk, e)
    kkn = kk * lax.rsqrt(jnp.maximum(ss, 1e-24))
    rksum = jnp.zeros_like(r)
    for d in range(2):
        xw = lora[:, 64 * d:64 * d + 64]
        xa = lora[:, 128 + 64 * d:128 + 64 * d + 64]
        w_log = -_softplus(-(w0_ref[d:d + 1, :] + _dot(jnp.tanh(xw), w2_ref[d]))) - 0.5
        lw_ref[d] = -jnp.exp(w_log)
        a_lr = jax.nn.sigmoid(a0_ref[d:d + 1, :] + _dot(xa, a2_ref[d]))
        kd = k * (1.0 + (a_lr - 1.0) * ka_ref[...])
        kd_ref[d] = kd
        bd_ref[d] = kkn * a_lr
        rksum = rksum + kd
    bonus_ref[...] = _dot_split(r * rksum * rk_ref[...], e) * v
    g_ref[...] = _dot(jax.nn.sigmoid(xg), g2_ref[...])
    r_ref[...] = r
    v_ref[...] = v
    a_ref[...] = -kkn


def rwkv_prep(z_a, lp, head_ones, tm):
    t = z_a.shape[0]
    w = RW_WIDTH
    tok = pl.BlockSpec((tm, w), lambda i: (i, 0))
    tok2 = pl.BlockSpec((2, tm, w), lambda i: (0, i, 0))
    full = lambda shp: pl.BlockSpec(shp, lambda i: (0,) * len(shp))
    one = jax.ShapeDtypeStruct((t, w), F32)
    two = jax.ShapeDtypeStruct((2, t, w), F32)
    return pl.pallas_call(
        _rwkv_prep_kernel, grid=(t // tm,),
        in_specs=[pl.BlockSpec((tm, RW_COLS), lambda i: (i, 0)),
                  full((2, w)), full((2, RW_LORA, w)), full((2, w)), full((2, RW_LORA, w)),
                  full((RW_GATE_LORA, w)), full((1, w)), full((1, w)), full((1, w)), full((w, w))],
        out_specs=[tok, tok, tok, tok2, tok2, tok2, tok, tok],
        out_shape=[one, one, one, two, two, two, one, one],
        compiler_params=_cp("parallel"), name="rwkv_prep",
    )(z_a, lp["rwkv_w0"], lp["rwkv_w2"], lp["rwkv_a0"], lp["rwkv_a2"], lp["rwkv_g2"],
      lp["rwkv_k_k"].reshape(1, w), lp["rwkv_k_a"].reshape(1, w), lp["rwkv_r_k"].reshape(1, w),
      head_ones)


def _rwkv_scan_kernel(r_ref, v_ref, a_ref, lw_ref, kd_ref, bd_ref, s0_ref, y_ref, sfin_ref, s_scr):
    d = pl.program_id(1)
    ci = pl.program_id(2)
    nc = pl.num_programs(2)
    c = RW_CHUNK
    n = RW_HEAD_DIM

    @pl.when(ci == 0)
    def _():
        s_scr[...] = s0_ref[...]

    row = lax.broadcasted_iota(jnp.int32, (c, c), 0)
    col = lax.broadcasted_iota(jnp.int32, (c, c), 1)
    fwd = d == 0
    ahead = (row - col) * jnp.where(fwd, 1, -1)
    strict = ahead > 0
    incl = ahead >= 0
    row2 = lax.broadcasted_iota(jnp.int32, (c, 2 * c), 0)
    col2 = lax.broadcasted_iota(jnp.int32, (c, 2 * c), 1) & (c - 1)
    incl2 = (row2 - col2) * jnp.where(fwd, 1, -1) >= 0
    incl_f = incl.astype(BF16)
    eye = (col == row).astype(F32)

    lw = lw_ref[...]
    cw = _dot_split3(incl_f, lw)
    cwx = cw - lw
    last = jnp.where(fwd, c - 1, 0)
    rowc = lax.broadcasted_iota(jnp.int32, (c, 1), 0)
    cw_end = jnp.sum(jnp.where(rowc == last, cw, 0.0), axis=0, keepdims=True)
    e_in = jnp.exp(cw)
    e_out = jnp.exp(-cw)
    rt = r_ref[...] * e_in
    at = a_ref[...] * jnp.exp(cwx)
    bt = bd_ref[...] * e_out
    kt = kd_ref[...] * e_out
    v = v_ref[...]
    w_end = jnp.exp(cw_end)

    hs = range(RW_HEADS)
    sls = [slice(h * n, (h + 1) * n) for h in hs]
    ar = [jnp.concatenate([at[:, sl], rt[:, sl]], axis=0).astype(BF16) for sl in sls]
    bk = [jnp.concatenate([bt[:, sl], kt[:, sl]], axis=0).astype(BF16) for sl in sls]
    vh = [v[:, sl] for sl in sls]
    s_old = [s_scr[h] for h in hs]
    g = [_dot_nt(ar[h], bk[h]) for h in hs]
    ars0 = [_dot_nt(ar[h], s_old[h]) for h in hs]
    ab = [jnp.where(strict, g[h][:c, :c], 0.0) for h in hs]
    ak = [jnp.where(strict, g[h][:c, c:], 0.0) for h in hs]
    rbk = [jnp.where(incl2, g[h][c:, :], 0.0).astype(BF16) for h in hs]
    rhs = [ars0[h][:c] + _dot(ak[h], vh[h]) for h in hs]
    p = [eye + ab[h] for h in hs]
    m = [ab[h].astype(BF16) for h in hs]
    for it in range(5):
        m2 = [jnp.dot(m[h], m[h], preferred_element_type=F32) for h in hs]
        m = [m2[h].astype(BF16) for h in hs]
        p = [p[h] + jnp.dot(p[h].astype(BF16), m[h], preferred_element_type=F32) for h in hs]
    u = [_dot(p[h], rhs[h]) for h in hs]
    uv = [jnp.concatenate([u[h], vh[h]], axis=0).astype(BF16) for h in hs]
    ys = [ars0[h][c:] + jnp.dot(rbk[h], uv[h], preferred_element_type=F32) for h in hs]
    for h in hs:
        s_scr[h] = (s_old[h] + _dot_tn(uv[h], bk[h])) * w_end[:, sls[h]]
    y_ref[...] = jnp.concatenate(ys, axis=1)

    @pl.when(ci == nc - 1)
    def _():
        sfin_ref[...] = s_scr[...]


def rwkv_scan(r, v, a, lw, kd, bd, s0, *, tok0, nseq, seq_len, out_buf=None):
    c = RW_CHUNK
    w = RW_WIDTH
    nch = seq_len // c
    blk0 = tok0 // c

    def chunk(s, d, ci):
        return blk0 + s * nch + jnp.where(d == 0, ci, nch - 1 - ci)

    shared = pl.BlockSpec((c, w), lambda s, d, ci: (chunk(s, d, ci), 0))
    own = pl.BlockSpec((None, c, w), lambda s, d, ci: (d, chunk(s, d, ci), 0))
    st = pl.BlockSpec((None, None, RW_HEADS, RW_HEAD_DIM, RW_HEAD_DIM), lambda s, d, ci: (s, d, 0, 0, 0))
    return _call_into(
        _rwkv_scan_kernel, (r, v, a, lw, kd, bd, s0), out_buf, grid=(nseq, 2, nch),
        in_specs=[shared, shared, shared, own, own, own, st],
        out_specs=[own, st],
        out_shape=[jax.ShapeDtypeStruct((2, r.shape[0], w), F32),
                   jax.ShapeDtypeStruct((nseq, 2, RW_HEADS, RW_HEAD_DIM, RW_HEAD_DIM), F32)],
        scratch_shapes=[pltpu.VMEM((RW_HEADS, RW_HEAD_DIM, RW_HEAD_DIM), F32)],
        compiler_params=_cp("parallel", "parallel", "arbitrary"), name="rwkv_scan")


def _rwkv_post_kernel(ys_ref, bonus_ref, g_ref, lnw_ref, lnb_ref, e_ref, o_ref):
    e = e_ref[...]
    y = ys_ref[0] + ys_ref[1]
    inv_n = 1.0 / RW_HEAD_DIM
    mean = _dot_split(y, e) * inv_n
    dlt = y - mean
    var = _dot_split(dlt * dlt, e) * inv_n
    yn = dlt * lax.rsqrt(var + RW_GN_EPS) * lnw_ref[...] + lnb_ref[...]
    o_ref[...] = ((yn + bonus_ref[...]) * g_ref[...]).astype(o_ref.dtype)


def rwkv_post(ys, bonus, g, lp, head_ones, tm):
    t, w = bonus.shape
    tok = pl.BlockSpec((tm, w), lambda i: (i, 0))
    return pl.pallas_call(
        _rwkv_post_kernel, grid=(t // tm,),
        in_specs=[pl.BlockSpec((2, tm, w), lambda i: (0, i, 0)), tok, tok,
                  pl.BlockSpec((1, w), lambda i: (0, 0)), pl.BlockSpec((1, w), lambda i: (0, 0)),
                  pl.BlockSpec((w, w), lambda i: (0, 0))],
        out_specs=tok, out_shape=jax.ShapeDtypeStruct((t, w), BF16),
        compiler_params=_cp("parallel"), name="rwkv_post",
    )(ys, bonus, g, lp["rwkv_ln_w"].reshape(1, w), lp["rwkv_ln_b"].reshape(1, w), head_ones)


def _mla_prep_kernel(z_ref, qg_ref, kvg_ref, qn_ref, ckv_ref):
    zq = z_ref[:, 0:MLA_RANK]
    zkv = z_ref[:, MLA_RANK:2 * MLA_RANK]
    qn = zq * lax.rsqrt(jnp.mean(zq * zq, axis=-1, keepdims=True) + NORM_EPS) * qg_ref[...]
    qn_ref[...] = qn.astype(qn_ref.dtype)
    ckv_ref[...] = zkv * lax.rsqrt(jnp.mean(zkv * zkv, axis=-1, keepdims=True) + NORM_EPS) * kvg_ref[...]


def mla_prep(z_a, lp, tm):
    t = z_a.shape[0]
    blk = RW_COLS // (COLS_A - RW_COLS)
    wb = COLS_A - RW_COLS
    return pl.pallas_call(
        _mla_prep_kernel, grid=(t // tm,),
        in_specs=[pl.BlockSpec((tm, wb), lambda i: (i, blk)),
                  pl.BlockSpec((1, MLA_RANK), lambda i: (0, 0)),
                  pl.BlockSpec((1, MLA_RANK), lambda i: (0, 0))],
        out_specs=[pl.BlockSpec((tm, MLA_RANK), lambda i: (i, 0)),
                   pl.BlockSpec((tm, MLA_RANK), lambda i: (i, 0))],
        out_shape=[jax.ShapeDtypeStruct((t, MLA_RANK), BF16), jax.ShapeDtypeStruct((t, MLA_RANK), F32)],
        compiler_params=_cp("parallel"), name="mla_prep",
    )(z_a, lp["mla_q_norm"].reshape(1, MLA_RANK), lp["mla_kv_norm"].reshape(1, MLA_RANK))


def _attn_kernel(qn_ref, qr_ref, kn_ref, kr_ref, v_ref, o_ref):
    scale = (MLA_NOPE + MLA_ROPE) ** -0.5
    qr = qr_ref[...].astype(F32)
    kr = kr_ref[...]
    lane = lax.broadcasted_iota(jnp.int32, (1, 128), 1)
    outs = []
    for hh in range(2):
        qn = qn_ref[:, hh * MLA_NOPE:(hh + 1) * MLA_NOPE]
        kn = kn_ref[:, hh * MLA_NOPE:(hh + 1) * MLA_NOPE]
        qr_h = jnp.where((lane >= 64 * hh) & (lane < 64 * hh + 64), qr, 0.0)
        s = (_dot_nt(qn, kn) + _dot_nt(qr_h, kr)) * scale
        s = s - jnp.max(s, axis=-1, keepdims=True)
        p = jnp.exp(s)
        p = p / jnp.sum(p, axis=-1, keepdims=True)
        outs.append(_dot(p, v_ref[:, hh * MLA_V:(hh + 1) * MLA_V]))
    o_ref[...] = jnp.concatenate(outs, axis=1).astype(o_ref.dtype)


def attention(q_all, qr2, kv_all, kr2, *, q_tok0, k_tok0, nseq, lq, lk, tq, out_buf=None):
    nq = lq // tq
    qb0 = q_tok0 // tq
    kb0 = k_tok0 // lk
    npair = MLA_HEADS // 2
    return _call_into(
        _attn_kernel, (q_all, qr2, kv_all, kr2, kv_all), out_buf, grid=(nseq, npair, nq),
        in_specs=[pl.BlockSpec((tq, 256), lambda b, p, i: (qb0 + b * nq + i, p)),
                  pl.BlockSpec((tq, 128), lambda b, p, i: (qb0 + b * nq + i, p)),
                  pl.BlockSpec((lk, 256), lambda b, p, i: (kb0 + b, p)),
                  pl.BlockSpec((lk, 128), lambda b, p, i: (kb0 + b, 0)),
                  pl.BlockSpec((lk, 256), lambda b, p, i: (kb0 + b, npair + p))],
        out_specs=pl.BlockSpec((tq, 256), lambda b, p, i: (qb0 + b * nq + i, p)),
        out_shape=jax.ShapeDtypeStruct((q_all.shape[0], MLA_HEADS * MLA_V), BF16),
        compiler_params=_cp("parallel", "parallel", "parallel"), name="attention")


def _s5_kernel(uf_ref, ub_ref, wb_ref, wc_ref, lam_ref, h0_ref, yf_ref, yb_ref, fin_ref,
               bre, bim, hre, him, *, sq):
    ci = pl.program_id(1)
    nc = pl.num_programs(1)
    ns = S5_STREAMS
    ts = S5_TSTEPS
    gl = 8 * S5_STATE

    @pl.when(ci == 0)
    def _():
        hre[...] = h0_ref[0]
        him[...] = h0_ref[1]

    def perm(transposed):
        shape = (sq * ts, ts * ns) if transposed else (ts * ns, sq * ts)
        r = lax.broadcasted_iota(jnp.int32, shape, 1 if transposed else 0)
        c = lax.broadcasted_iota(jnp.int32, shape, 0 if transposed else 1)
        step_i, strm = r >> (ns.bit_length() - 1), r & (ns - 1)
        seq, tim = c >> (ts.bit_length() - 1), c & (ts - 1)
        return (((strm == seq) & (tim == step_i)).astype(BF16),
                ((strm == seq + sq) & (tim == ts - 1 - step_i)).astype(BF16))

    pf, pb = perm(False)
    pf_t, pb_t = perm(True)

    uf = uf_ref[...].reshape(sq * ts, S5_WIDTH).astype(BF16)
    ub = ub_ref[...].reshape(sq * ts, S5_WIDTH).astype(BF16)
    u8f = jnp.dot(pf, uf, preferred_element_type=F32).astype(BF16)
    u8b = jnp.dot(pb, ub, preferred_element_type=F32).astype(BF16)
    for k in range(S5_GROUPS // 8):
        ks = slice(128 * k, 128 * k + 128)
        bu = jnp.dot(jnp.concatenate([u8f[:, ks], u8b[:, ks]], axis=1), wb_ref[k],
                     preferred_element_type=F32)
        bre[:, gl * k:gl * k + gl] = bu[:, :gl]
        bim[:, gl * k:gl * k + gl] = bu[:, gl:]

    lw = 1024
    sub = lax.broadcasted_iota(jnp.int32, (ns, lw), 0)
    for q in range(S5_LANES // lw):
        ls = slice(q * lw, (q + 1) * lw)
        lr = jnp.where(sub < sq, lam_ref[0, 0:1, ls], lam_ref[1, 0:1, ls])
        li = jnp.where(sub < sq, lam_ref[0, 1:2, ls], lam_ref[1, 1:2, ls])

        hr, hi = hre[:, ls], him[:, ls]
        for t in range(ts):
            rows = slice(t * ns, (t + 1) * ns)
            nr = lr * hr - li * hi + bre[rows, ls]
            ni = lr * hi + li * hr + bim[rows, ls]
            bre[rows, ls] = nr
            bim[rows, ls] = ni
            hr, hi = nr, ni
        hre[:, ls] = hr
        him[:, ls] = hi

    y8 = []
    for k in range(S5_GROUPS // 8):
        hk = jnp.concatenate([bre[:, gl * k:gl * k + gl], bim[:, gl * k:gl * k + gl]], axis=1).astype(BF16)
        y8.append(jnp.dot(hk, wc_ref[k], preferred_element_type=F32))
    for d, (pm, y_ref) in enumerate(((pf_t, yf_ref), (pb_t, yb_ref))):
        yd = jnp.concatenate([y[:, 128 * d:128 * d + 128] for y in y8], axis=1)
        y_ref[...] = _dot_hi_lo(pm, yd).reshape(sq, ts, S5_WIDTH)

    @pl.when(ci == nc - 1)
    def _():
        fin_ref[0] = hre[...]
        fin_ref[1] = him[...]


def s5_scan(z_seq, wb, wc, lam, h0, sq):
    nseq, seq_len, _ = z_seq.shape
    ng, nch = nseq // sq, seq_len // S5_TSTEPS
    rb = S5_TSTEPS * S5_STREAMS
    blk_f = pl.BlockSpec((sq, S5_TSTEPS, S5_WIDTH), lambda g, c: (g, c, 0))
    blk_b = pl.BlockSpec((sq, S5_TSTEPS, S5_WIDTH), lambda g, c: (g, nch - 1 - c, 0))
    st = pl.BlockSpec((None, 2, S5_STREAMS, S5_LANES), lambda g, c: (g, 0, 0, 0))
    yshape = jax.ShapeDtypeStruct((nseq, seq_len, S5_WIDTH), F32)
    return pl.pallas_call(
        functools.partial(_s5_kernel, sq=sq), grid=(ng, nch),
        in_specs=[blk_f, blk_b,
                  pl.BlockSpec((8, 256, 1024), lambda g, c: (0, 0, 0)),
                  pl.BlockSpec((8, 1024, 256), lambda g, c: (0, 0, 0)),
                  pl.BlockSpec((2, 2, S5_LANES), lambda g, c: (0, 0, 0)), st],
        out_specs=[blk_f, blk_b, st],
        out_shape=[yshape, yshape, jax.ShapeDtypeStruct((ng, 2, S5_STREAMS, S5_LANES), F32)],
        scratch_shapes=[pltpu.VMEM((rb, S5_LANES), F32), pltpu.VMEM((rb, S5_LANES), F32),
                        pltpu.VMEM((S5_STREAMS, S5_LANES), F32), pltpu.VMEM((S5_STREAMS, S5_LANES), F32)],
        compiler_params=_cp("parallel", "arbitrary"), name="s5_scan",
    )(z_seq, z_seq, wb, wc, lam, h0)


def _s5_glu_kernel(yf_ref, yb_ref, z_ref, d_ref, w1_ref, w2_ref, o_ref, y_scr):
    @pl.when(pl.program_id(1) == 0)
    def _():
        y = yf_ref[...] + yb_ref[...] + d_ref[...] * z_ref[...]
        y_scr[...] = jax.nn.gelu(y).astype(BF16)

    y = y_scr[...]
    o_ref[...] = (_dot(y, w1_ref[...]) * jax.nn.sigmoid(_dot(y, w2_ref[...]))).astype(o_ref.dtype)


def s5_glu(yf, yb, z_s5, lp, tm, tn, *, tok0, t_total, out_buf=None):
    t, w = z_s5.shape
    i0 = tok0 // tm
    tok = pl.BlockSpec((tm, w), lambda i, j: (i, 0))
    wsp = pl.BlockSpec((w, tn), lambda i, j: (0, j))
    args = (yf, yb, z_s5, lp["s5_d"].reshape(1, w), lp["s5_glu_w1"], lp["s5_glu_w2"])
    return _call_into(
        _s5_glu_kernel, args, out_buf, grid=(t // tm, w // tn),
        in_specs=[tok, tok, tok, pl.BlockSpec((1, w), lambda i, j: (0, 0)), wsp, wsp],
        out_specs=pl.BlockSpec((tm, tn), lambda i, j: (i0 + i, j)),
        out_shape=jax.ShapeDtypeStruct((t_total, w), BF16),
        scratch_shapes=[pltpu.VMEM((tm, w), BF16)],
        compiler_params=_cp("parallel", "arbitrary"), name="s5_glu")


def _merge_kernel(a_ref, b_ref, c_ref, wa_ref, wb_ref, wc_ref, ga_ref, gb_ref, gc_ref, o_ref):
    acc = ga_ref[...].astype(F32) * _dot(a_ref[...], wa_ref[...])
    acc += gb_ref[...].astype(F32) * _dot(b_ref[...], wb_ref[...])
    acc += gc_ref[...].astype(F32) * _dot(c_ref[...], wc_ref[...])
    o_ref[...] = acc.astype(o_ref.dtype)


def merge(rw_out, mla_out, s5_out, gates, lp, tm, tn):
    t, k = rw_out.shape
    d = D_MODEL
    nb = d // tn
    tok = pl.BlockSpec((tm, k), lambda i, j: (i, 0))
    wsp = pl.BlockSpec((k, tn), lambda i, j: (0, j))
    gsp = lambda br: pl.BlockSpec((tm, tn), lambda i, j: (i, br * nb + j))
    return pl.pallas_call(
        _merge_kernel, grid=(t // tm, nb),
        in_specs=[tok, tok, tok, wsp, wsp, wsp, gsp(0), gsp(1), gsp(2)],
        out_specs=pl.BlockSpec((tm, tn), lambda i, j: (i, j)),
        out_shape=jax.ShapeDtypeStruct((t, d), BF16),
        compiler_params=_cp("parallel", "parallel"), name="merge",
    )(rw_out, mla_out, s5_out, lp["w_branch_rwkv"], lp["w_branch_mla"], lp["w_branch_s5"],
      gates, gates, gates)


def _top16(tiles):
    n = PEER_TOPK
    v = list(tiles)
    k = 2
    while k <= n:
        j = k // 2
        while j >= 1:
            for i in range(n):
                l = i ^ j
                if l > i:
                    hi, lo = jnp.maximum(v[i], v[l]), jnp.minimum(v[i], v[l])
                    v[i], v[l] = (hi, lo) if (i & k) == 0 else (lo, hi)
            j //= 2
        k *= 2
    for dist in (1, 2, 4):
        v = [jnp.maximum(v[i], pltpu.roll(v[n - 1 - i], dist, 0)) for i in range(n)]
        j = n // 2
        while j >= 1:
            for i in range(n):
                if (i & j) == 0:
                    v[i], v[i + j] = jnp.maximum(v[i], v[i + j]), jnp.minimum(v[i], v[i + j])
            j //= 2
    return v


def _rows_from(v, sub):
    out = v[0]
    for r in range(1, 8):
        out = jnp.where(sub == r, v[r], out)
    return out


def _peer_topk_kernel(q_ref, keys_ref, s1_ref, e1_ref, s2_ref, tau_ref, m2_ref):
    q = q_ref[...]
    s1 = _dot_nt(keys_ref[0], q[:, :PEER_HALF])
    s2 = _dot_nt(keys_ref[1], q[:, PEER_HALF:])
    tiles = lambda s: [s[8 * i:8 * i + 8, :] for i in range(s.shape[0] // 8)]
    sv1 = _top16(tiles(s1))
    sv2 = _top16(tiles(s2))
    sub = lax.broadcasted_iota(jnp.int32, sv1[0].shape, 0)
    half = PEER_TOPK // 2
    sv2_lo, sv2_hi, sv1_hi = _rows_from(sv2[:half], sub), _rows_from(sv2[half:], sub), _rows_from(sv1[half:], sub)
    cand = ([sv1[0] + sv2_lo, sv1[0] + sv2_hi] + [sv1[a] + sv2_lo for a in range(1, half)]
            + [sv1_hi + sv2[0]])
    cand += [jnp.full_like(cand[0], -jnp.inf)] * (PEER_TOPK - len(cand))
    cv = [c[0:1] for c in _top16(cand)]
    z = jnp.zeros_like(cv[0])
    for c in cv:
        z = z + jnp.exp(c - cv[0])
    s1_ref[...] = s1
    s2_ref[...] = s2
    e1_ref[...] = jnp.exp(s1 - sv1[0][0:1]) / z
    tau_ref[...] = cv[PEER_TOPK - 1]
    m2_ref[...] = sv2[0][0:1]


def peer_topk(q, keys, tt):
    t = q.shape[0]
    hd = PEER_HEADS
    big = pl.BlockSpec((None, PEER_KEYS, tt), lambda i, h: (h, 0, i))
    small = pl.BlockSpec((None, 1, tt), lambda i, h: (h, 0, i))
    bshape = jax.ShapeDtypeStruct((hd, PEER_KEYS, t), F32)
    sshape = jax.ShapeDtypeStruct((hd, 1, t), F32)
    return pl.pallas_call(
        _peer_topk_kernel, grid=(t // tt, hd),
        in_specs=[pl.BlockSpec((tt, 2 * PEER_HALF), lambda i, h: (i, h)),
                  pl.BlockSpec((None, 2, PEER_KEYS, PEER_HALF), lambda i, h: (h, 0, 0, 0))],
        out_specs=[big, big, big, small, small],
        out_shape=[bshape, bshape, bshape, sshape, sshape],
        compiler_params=_cp("parallel", "parallel"), name="peer_topk",
    )(q, keys)


def _peer_dense_kernel(ht_ref, u_ref, v_ref, s1_ref, e1_ref, s2_ref, tau_ref, m2_ref, o_ref, e2_scr,
                       *, n_i1):
    j = pl.program_id(1)

    @pl.when(j == 0)
    def _():
        o_ref[...] = jnp.zeros_like(o_ref)
        e2_scr[...] = jnp.exp(s2_ref[...] - m2_ref[...])

    def gate_rows(a):
        g = None
        for h in range(PEER_HEADS):
            row = pl.ds(j * n_i1 + a, 1)
            sm = s1_ref[h, row, :] + s2_ref[h]
            term = jnp.where(sm >= tau_ref[h], e1_ref[h, row, :] * e2_scr[h], 0.0)
            g = term if g is None else g + term
        return g

    gate = jnp.concatenate([gate_rows(a) for a in range(n_i1)], axis=0)
    sc = _dot(u_ref[...], ht_ref[...])
    act = jax.nn.gelu(sc.astype(BF16)) * gate.astype(BF16)
    o_ref[...] += _dot_tn(act, v_ref[...])


def peer_dense(h2t, u_tab, v_tab, layer, s1, e1, s2, tau, m2, tt, te):
    d, t = h2t.shape
    ne = u_tab.shape[1]
    n_i1 = te // PEER_KEYS
    hd = PEER_HEADS
    return pl.pallas_call(
        functools.partial(_peer_dense_kernel, n_i1=n_i1), grid=(t // tt, ne // te),
        in_specs=[pl.BlockSpec((d, tt), lambda i, j: (0, i), pipeline_mode=pl.Buffered(1)),
                  pl.BlockSpec((None, te, d), lambda i, j: (layer, j, 0)),
                  pl.BlockSpec((None, te, d), lambda i, j: (layer, j, 0)),
                  pl.BlockSpec((hd, PEER_KEYS, tt), lambda i, j: (0, 0, i), pipeline_mode=pl.Buffered(1)),
                  pl.BlockSpec((hd, PEER_KEYS, tt), lambda i, j: (0, 0, i), pipeline_mode=pl.Buffered(1)),
                  pl.BlockSpec((hd, PEER_KEYS, tt), lambda i, j: (0, 0, i), pipeline_mode=pl.Buffered(1)),
                  pl.BlockSpec((hd, 1, tt), lambda i, j: (0, 0, i)),
                  pl.BlockSpec((hd, 1, tt), lambda i, j: (0, 0, i))],
        out_specs=pl.BlockSpec((tt, d), lambda i, j: (i, 0)),
        out_shape=jax.ShapeDtypeStruct((t, d), F32),
        scratch_shapes=[pltpu.VMEM((hd, PEER_KEYS, tt), F32)],
        compiler_params=_cp("parallel", "arbitrary"), name="peer_dense",
    )(h2t, u_tab, v_tab, s1, e1, s2, tau, m2)


def _rope_swap_perm():
    idx = []
    for ax in range(2):
        base = ax * 2 * ROPE_FREQS
        idx += list(range(base + ROPE_FREQS, base + 2 * ROPE_FREQS)) + list(range(base, base + ROPE_FREQS))
    return jnp.array(idx, jnp.int32)


def _rope_tables(n_tokens):
    rows = n_tokens // GRID_W
    r_idx, c_idx = jnp.meshgrid(jnp.arange(rows), jnp.arange(GRID_W), indexing="ij")
    pos = jnp.stack([r_idx.reshape(-1), c_idx.reshape(-1)], axis=-1).astype(F32)
    inv = ROPE_BASE ** (-jnp.arange(ROPE_FREQS, dtype=F32) / ROPE_FREQS)
    ang = pos[:, :, None] * inv
    cos, sin = jnp.cos(ang), jnp.sin(ang)
    ct = jnp.concatenate([cos[:, 0], cos[:, 0], cos[:, 1], cos[:, 1]], axis=-1)
    st = jnp.concatenate([-sin[:, 0], sin[:, 0], -sin[:, 1], sin[:, 1]], axis=-1)
    return ct, st


def _s5_params(lp):
    f32 = F32
    a_re, a_im = lp["s5_a_re"].astype(f32), lp["s5_a_im"].astype(f32)
    dt = jnp.exp(lp["s5_log_step"].astype(f32))[..., None]
    mag = jnp.exp(a_re * dt)
    lb_re, lb_im = mag * jnp.cos(a_im * dt), mag * jnp.sin(a_im * dt)
    den = a_re * a_re + a_im * a_im
    nr, ni = lb_re - 1.0, lb_im
    coef_re = (nr * a_re + ni * a_im) / den
    coef_im = (ni * a_re - nr * a_im) / den
    b_re, b_im = lp["s5_b_re"].astype(f32), lp["s5_b_im"].astype(f32)
    bb_re = coef_re[..., None] * b_re - coef_im[..., None] * b_im
    bb_im = coef_re[..., None] * b_im + coef_im[..., None] * b_re
    eye = jnp.eye(8, dtype=f32)

    def blockdiag_in(bb):
        t = bb.reshape(2, 8, 8, S5_STATE, S5_CH)
        return jnp.einsum("dkgpc,gh->dkgchp", t, eye).reshape(2, 8, 8 * S5_CH, 8 * S5_STATE)

    def blockdiag_out(cc):
        t = cc.reshape(2, 8, 8, S5_CH, S5_STATE)
        return jnp.einsum("dkgcp,gh->dkgphc", t, eye).reshape(2, 8, 8 * S5_STATE, 8 * S5_CH)

    wb = jnp.concatenate([blockdiag_in(bb_re), blockdiag_in(bb_im)], axis=-1)
    wc = jnp.concatenate([blockdiag_out(lp["s5_c_re"].astype(f32)),
                          -blockdiag_out(lp["s5_c_im"].astype(f32))], axis=-2)
    lam = jnp.stack([lb_re.reshape(2, S5_LANES), lb_im.reshape(2, S5_LANES)], axis=1)
    wb = jnp.concatenate([wb[0], wb[1]], axis=1)
    wc = jnp.concatenate([wc[0], wc[1]], axis=2)
    return wb.astype(BF16), wc.astype(BF16), lam


def _s5_state_groups(s_re, s_im, sq):
    nseq = s_re.shape[0]

    def arrange(s):
        s = s.astype(F32).reshape(nseq // sq, sq, 2, S5_LANES).transpose(0, 2, 1, 3)
        s = s.reshape(nseq // sq, 2 * sq, S5_LANES)
        pad = S5_STREAMS - 2 * sq
        if pad:
            s = jnp.concatenate([s, jnp.zeros((nseq // sq, pad, S5_LANES), F32)], axis=1)
        return s

    return jnp.stack([arrange(s_re), arrange(s_im)], axis=1)


def _layer(x, h, l, lp, mod, row_of_tile, groups, caches, consts):
    t = x.shape[0]
    (n_ctx, l_ctx), (n_lat, l_lat) = groups
    t_ctx = n_ctx * l_ctx
    head_ones, rope_ct, rope_st, swap = consts
    w_in_t = lp["w_in_t"]

    mu_pad = jnp.concatenate([lp["rwkv_mu"], jnp.zeros((COLS_A - RW_COLS,), F32)]).reshape(1, COLS_A)
    z_a = matmul_in_a(h, w_in_t, l, mu_pad, n_ctx_tiles=t_ctx // TM, l_ctx=l_ctx, l_lat=l_lat,
                      tm=TM, tn=COLS_A // 4)
    w_kr_t = w_in_t[l, S5_COL0 - MLA_ROPE:S5_COL0]
    kr2 = matmul(h, jnp.concatenate([w_kr_t, w_kr_t[swap]], axis=0).T, tm=TM, tn=128, out_dtype=F32,
                 name="matmul_kr")
    z_s5_c = matmul_wt(h, w_in_t, l, S5_COL0, S5_WIDTH, tm=TM, tn=512, out_dtype=F32, rows=t_ctx,
                       name="matmul_s5in")
    z_s5_l = matmul_wt(h, w_in_t, l, S5_COL0, S5_WIDTH, tm=TM, tn=512, out_dtype=F32, row0=t_ctx,
                       rows=t - t_ctx, name="matmul_s5in")
    gates = matmul_wt(h, w_in_t, l, GATE_COL0, 3 * D_MODEL, tm=TM, tn=1024, out_dtype=BF16, act="sigmoid",
                      name="matmul_gates")

    r, v, a, lw, kd, bd, bonus, g = rwkv_prep(z_a, lp, head_ones, 256)
    rw0_ctx = jnp.zeros((n_ctx, 2, RW_HEADS, RW_HEAD_DIM, RW_HEAD_DIM), F32)
    ys, rw_fin = rwkv_scan(r, v, a, lw, kd, bd, rw0_ctx, tok0=0, nseq=n_ctx, seq_len=l_ctx)
    ys, _ = rwkv_scan(r, v, a, lw, kd, bd, caches["rwkv"], tok0=t_ctx, nseq=n_lat, seq_len=l_lat, out_buf=ys)
    rw_out = rwkv_post(ys, bonus, g, lp, head_ones, 512)

    qn, ckv = mla_prep(z_a, lp, TM)
    w_uq = lp["mla_w_uq"].reshape(MLA_RANK, MLA_HEADS, MLA_NOPE + MLA_ROPE)
    w_qr = w_uq[:, :, MLA_NOPE:]
    w_q = jnp.concatenate([w_uq[:, :, :MLA_NOPE].reshape(MLA_RANK, -1), w_qr.reshape(MLA_RANK, -1),
                           w_qr[:, :, swap].reshape(MLA_RANK, -1)], axis=1)
    q_all = matmul(qn, w_q, tm=TM, tn=1024, out_dtype=BF16, name="matmul_q")
    q_rope = q_all[:, 1024:1536].astype(F32)
    q_rope_sw = q_all[:, 1536:2048].astype(F32)
    ct8 = jnp.tile(rope_ct, (n_lat, MLA_HEADS))
    st8 = jnp.tile(rope_st, (n_lat, MLA_HEADS))
    qr_lat = q_rope[t_ctx:] * ct8 + q_rope_sw[t_ctx:] * st8
    qr2 = jnp.concatenate([q_rope[:t_ctx], qr_lat], axis=0).astype(BF16)
    kr_raw = kr2[:, :MLA_ROPE]
    kr_lat = (kr_raw[t_ctx:] * jnp.tile(rope_ct, (n_lat, 1))
              + kr2[t_ctx:, MLA_ROPE:] * jnp.tile(rope_st, (n_lat, 1)))
    past = caches["ckv"].shape[1]
    ckv_lat_all = jnp.concatenate([caches["ckv"], ckv[t_ctx:].reshape(n_lat, l_lat, MLA_RANK)], axis=1)
    kr_lat_all = jnp.concatenate([caches["krope"], kr_lat.reshape(n_lat, l_lat, MLA_ROPE)], axis=1)
    ckv_all = jnp.concatenate([ckv_lat_all.reshape(-1, MLA_RANK), ckv[:t_ctx]], axis=0)
    kr_all = jnp.concatenate([kr_lat_all.reshape(-1, MLA_ROPE), kr_raw[:t_ctx]], axis=0)
    kr_dup = jnp.concatenate([kr_all, kr_all], axis=1).astype(BF16)
    w_ukv = jnp.concatenate([lp["mla_w_uk"], lp["mla_w_uv"]], axis=1)
    kv_all = matmul(ckv_all.astype(BF16), w_ukv, tm=512, tn=1024, out_dtype=BF16, name="matmul_kv")
    lk_lat = past + l_lat
    mla_out = attention(q_all, qr2, kv_all, kr_dup, q_tok0=0, k_tok0=n_lat * lk_lat, nseq=n_ctx, lq=l_ctx,
                        lk=l_ctx, tq=l_ctx)
    mla_out = attention(q_all, qr2, kv_all, kr_dup, q_tok0=t_ctx, k_tok0=0, nseq=n_lat, lq=l_lat,
                        lk=lk_lat, tq=256, out_buf=mla_out)

    wb, wc, lam = lp["s5_pre"]
    sq_c, sq_l = min(n_ctx, S5_STREAMS // 2), min(n_lat, S5_STREAMS // 2)
    zero_c = jnp.zeros((n_ctx // sq_c, 2, S5_STREAMS, S5_LANES), F32)
    yf_c, yb_c, fin_c = s5_scan(z_s5_c.reshape(n_ctx, l_ctx, S5_WIDTH), wb, wc, lam, zero_c, sq_c)
    h0_l = _s5_state_groups(caches["s5_re"], caches["s5_im"], sq_l)
    yf_l, yb_l, _ = s5_scan(z_s5_l.reshape(n_lat, l_lat, S5_WIDTH), wb, wc, lam, h0_l, sq_l)
    s5_out = s5_glu(yf_c.reshape(t_ctx, S5_WIDTH), yb_c.reshape(t_ctx, S5_WIDTH), z_s5_c, lp, TM, 512,
                    tok0=0, t_total=t)
    s5_out = s5_glu(yf_l.reshape(t - t_ctx, S5_WIDTH), yb_l.reshape(t - t_ctx, S5_WIDTH), z_s5_l, lp, TM, 512,
                    tok0=t_ctx, t_total=t, out_buf=s5_out)
    fin = fin_c[:, :, :2 * sq_c].reshape(n_ctx // sq_c, 2, 2, sq_c, S5_GROUPS, S5_STATE)
    fin = fin.transpose(1, 0, 3, 2, 4, 5).reshape(2, n_ctx, 2, S5_GROUPS, S5_STATE)

    merged = merge(rw_out, mla_out, s5_out, gates, lp, TM, 512)
    attn_delta = matmul(merged, lp["w_out"], tm=TM, tn=1024, out_dtype=F32, name="matmul_out")
    x, h2 = resid_norm_mod(x, attn_delta, mod, 2, lp["norm2_g"], mod, row_of_tile, 4, 3, BF16, 512)

    q = matmul(h2, lp["peer_wq"], tm=TM, tn=1024, out_dtype=BF16, name="matmul_peer_q")
    s1, e1, s2, tau, m2 = peer_topk(q, lp["peer_keys"], TOPK_TT)
    peer_out = peer_dense(h2.T, lp["peer_u_all"], lp["peer_v_all"], l, s1, e1, s2, tau, m2, PEER_TT, PEER_TE)

    new = dict(ckv=ckv[:t_ctx].reshape(n_ctx, l_ctx, MLA_RANK),
               krope=kr_raw[:t_ctx].reshape(n_ctx, l_ctx, MLA_ROPE),
               rwkv=rw_fin, s5_re=fin[0], s5_im=fin[1])
    return x, peer_out, new


def _forward(x_prompt, x_sample, cache_mla_ckv, cache_mla_krope, state_rwkv, state_s5_re, state_s5_im,
             c, c_ctx, p):
    n_ctx, l_ctx, d = x_prompt.shape
    n_lat, l_lat, _ = x_sample.shape
    depth = p["ada_w"].shape[0]
    t_ctx = n_ctx * l_ctx
    groups = ((n_ctx, l_ctx), (n_lat, l_lat))

    cond8 = jnp.concatenate([c_ctx[None, :], c, jnp.zeros((8 - 1 - n_lat, d), F32)], axis=0)
    mod_all = ada_mod(cond8, p["ada_w"], p["ada_b"])

    def row_of_tile_for(tm):
        nct = t_ctx // tm
        per = l_lat // tm
        return lambda i: jnp.where(i < nct, 0, 1 + (i - nct) // per)

    head_ones = jnp.kron(jnp.eye(RW_HEADS, dtype=F32), jnp.ones((RW_HEAD_DIM, RW_HEAD_DIM), F32)).astype(BF16)
    rope_ct, rope_st = _rope_tables(l_lat)
    consts = (head_ones, rope_ct, rope_st, _rope_swap_perm())

    x = jnp.concatenate([x_prompt.reshape(t_ctx, d), x_sample.reshape(n_lat * l_lat, d)], axis=0)
    per_layer = ["norm1_g", "norm2_g", "rwkv_mu", "rwkv_w0", "rwkv_w2", "rwkv_a0", "rwkv_a2", "rwkv_g2",
                 "rwkv_k_k", "rwkv_k_a", "rwkv_r_k", "rwkv_ln_w", "rwkv_ln_b", "mla_q_norm", "mla_w_uq",
                 "mla_kv_norm", "mla_w_uk", "mla_w_uv", "s5_a_re", "s5_a_im", "s5_log_step", "s5_b_re",
                 "s5_b_im", "s5_c_re", "s5_c_im", "s5_d", "s5_glu_w1", "s5_glu_w2", "w_branch_rwkv",
                 "w_branch_mla", "w_branch_s5", "w_out", "peer_wq", "peer_keys"]
    w_in_t = jnp.swapaxes(p["w_in"], 1, 2)
    outs = dict(ckv=[], krope=[], rwkv=[], s5_re=[], s5_im=[])
    peer_out = None
    for l in range(depth):
        lp = {k: p[k][l] for k in per_layer}
        lp["w_in_t"] = w_in_t
        lp["peer_u_all"] = p["peer_u"]
        lp["peer_v_all"] = p["peer_v"]
        lp["s5_pre"] = _s5_params(lp)
        mod = mod_all[l].reshape(8, 1, 6 * d)
        rot = row_of_tile_for(512)
        if l == 0:
            h = norm_mod(x, lp["norm1_g"], mod, rot, 1, 0, BF16, 512)
        else:
            prev_mod = mod_all[l - 1].reshape(8, 1, 6 * d)
            x, h = resid_norm_mod(x, peer_out, prev_mod, 5, lp["norm1_g"], mod, rot, 1, 0, BF16, 512)
        caches = dict(ckv=cache_mla_ckv[:, l], krope=cache_mla_krope[:, l], rwkv=state_rwkv[:, l],
                      s5_re=state_s5_re[:, l], s5_im=state_s5_im[:, l])
        x, peer_out, new = _layer(x, h, l, lp, mod, rot, groups, caches, consts)
        for k in outs:
            outs[k].append(new[k])
    last_mod = mod_all[depth - 1].reshape(8, 1, 6 * d)
    zero_mod = jnp.zeros((8, 1, 6 * d), F32)
    _, y = resid_norm_mod(x, peer_out, last_mod, 5, p["final_norm_g"], zero_mod, row_of_tile_for(512), 0, 0,
                          F32, 512)
    y_prompt = y[:t_ctx].reshape(n_ctx, l_ctx, d)
    y_sample = y[t_ctx:].reshape(n_lat, l_lat, d)
    return (y_prompt, y_sample, jnp.stack(outs["ckv"], axis=1), jnp.stack(outs["krope"], axis=1),
            jnp.stack(outs["rwkv"], axis=1), jnp.stack(outs["s5_re"], axis=1), jnp.stack(outs["s5_im"], axis=1))


def kernel(x_prompt, x_sample, cache_mla_ckv, cache_mla_krope, state_rwkv, state_s5_re, state_s5_im, c, c_ctx, ada_w, ada_b, norm1_g, norm2_g, w_in, rwkv_mu, rwkv_w0, rwkv_w2, rwkv_a0, rwkv_a2, rwkv_g2, rwkv_k_k, rwkv_k_a, rwkv_r_k, rwkv_ln_w, rwkv_ln_b, mla_q_norm, mla_w_uq, mla_kv_norm, mla_w_uk, mla_w_uv, s5_a_re, s5_a_im, s5_log_step, s5_b_re, s5_b_im, s5_c_re, s5_c_im, s5_d, s5_glu_w1, s5_glu_w2, w_branch_rwkv, w_branch_mla, w_branch_s5, w_out, peer_wq, peer_keys, peer_u, peer_v, final_norm_g):
    p = dict(ada_w=ada_w, ada_b=ada_b, norm1_g=norm1_g, norm2_g=norm2_g, w_in=w_in, rwkv_mu=rwkv_mu,
             rwkv_w0=rwkv_w0, rwkv_w2=rwkv_w2, rwkv_a0=rwkv_a0, rwkv_a2=rwkv_a2, rwkv_g2=rwkv_g2,
             rwkv_k_k=rwkv_k_k, rwkv_k_a=rwkv_k_a, rwkv_r_k=rwkv_r_k, rwkv_ln_w=rwkv_ln_w,
             rwkv_ln_b=rwkv_ln_b, mla_q_norm=mla_q_norm, mla_w_uq=mla_w_uq, mla_kv_norm=mla_kv_norm,
             mla_w_uk=mla_w_uk, mla_w_uv=mla_w_uv, s5_a_re=s5_a_re, s5_a_im=s5_a_im,
             s5_log_step=s5_log_step, s5_b_re=s5_b_re, s5_b_im=s5_b_im, s5_c_re=s5_c_re, s5_c_im=s5_c_im,
             s5_d=s5_d, s5_glu_w1=s5_glu_w1, s5_glu_w2=s5_glu_w2, w_branch_rwkv=w_branch_rwkv,
             w_branch_mla=w_branch_mla, w_branch_s5=w_branch_s5, w_out=w_out, peer_wq=peer_wq,
             peer_keys=peer_keys, peer_u=peer_u, peer_v=peer_v, final_norm_g=final_norm_g)
    return _forward(x_prompt, x_sample, cache_mla_ckv, cache_mla_krope, state_rwkv, state_s5_re,
                    state_s5_im, c, c_ctx, p)
```
